```python
import jax, jax.numpy as jnp
from jax import lax
import numpy as np

D_MODEL = 1024
BATCH = 8
SEQ = 8192
DEPTH = 2

HEAD_DIM = 64
A_GROUPS = ((128, 1), (512, 4), (2048, 16))
N_GROUPS = len(A_GROUPS)
A_WIDTH = D_MODEL // 2
A_HEADS = A_WIDTH // HEAD_DIM
B_WIDTH = D_MODEL // 2
SC_WIDTH = 3
POOL_SIZES = (2, 4, 8, 16)
C_WIDTH = D_MODEL // 2
C_GROUP = C_WIDTH // len(POOL_SIZES)
D_WIDTH = D_MODEL // 2
D_CONV = 31
GATE_EVEN = A_WIDTH + B_WIDTH
GATE_ODD = C_WIDTH + D_WIDTH
EVEN_IN = 3 * N_GROUPS * A_WIDTH + 3 * B_WIDTH + GATE_EVEN
ODD_IN = C_WIDTH + 2 * D_WIDTH + GATE_ODD
ROT_DIM = HEAD_DIM // 4
ROPE_THETA = 500000.0
QBLK = 128
EPS = 1e-6
NEG = -1e30
N_EVEN = (DEPTH + 1) // 2
N_ODD = DEPTH // 2

kernel_name = "hybrid_dilated_attn_shortconv_pool_conformer"


def rms_norm(t, w):
    tf = t.astype(jnp.float32)
    tf = tf * lax.rsqrt(jnp.mean(tf * tf, axis=-1, keepdims=True) + EPS)
    return (tf * w.astype(jnp.float32)).astype(t.dtype)


def layer_norm(t, w, b):
    tf = t.astype(jnp.float32)
    mu = jnp.mean(tf, axis=-1, keepdims=True)
    var = jnp.mean(jnp.square(tf - mu), axis=-1, keepdims=True)
    y = (tf - mu) * lax.rsqrt(var + EPS)
    return (y * w.astype(jnp.float32) + b.astype(jnp.float32)).astype(t.dtype)


def rope_tables(positions):
    half = ROT_DIM // 2
    inv_freq = ROPE_THETA ** (-jnp.arange(half, dtype=jnp.float32) / half)
    ang = positions.astype(jnp.float32)[..., None] * inv_freq
    return jnp.cos(ang)[:, :, None, None, :], jnp.sin(ang)[:, :, None, None, :]


def apply_rope(t, cos, sin):
    half = ROT_DIM // 2
    t1 = t[..., :half].astype(jnp.float32)
    t2 = t[..., half:ROT_DIM].astype(jnp.float32)
    r1 = (t1 * cos - t2 * sin).astype(t.dtype)
    r2 = (t2 * cos + t1 * sin).astype(t.dtype)
    return jnp.concatenate([r1, r2, t[..., ROT_DIM:]], axis=-1)


def causal_dwconv(t, w):
    K, C = w.shape
    return lax.conv_general_dilated(
        t, w[:, None, :].astype(t.dtype), window_strides=(1,), padding=[(K - 1, 0)],
        dimension_numbers=('NWC', 'WIO', 'NWC'), feature_group_count=C)


def dilated_attention(q, k, v, window, dilation):
    Bsz, S, H, Dh = q.shape
    steps = window // dilation
    span = dilation * QBLK
    L = -(-S // span) * span
    n = L // dilation
    nb = n // QBLK

    def to_streams(t):
        t = jnp.pad(t, ((0, 0), (0, L - S), (0, 0), (0, 0)))
        t = t.reshape(Bsz, n, dilation, H, Dh).transpose(0, 2, 3, 1, 4)
        return t.reshape(Bsz, dilation, H, nb, QBLK, Dh)

    def with_prev(t):
        prev = jnp.pad(t, ((0, 0), (0, 0), (0, 0), (1, 0), (0, 0), (0, 0)))[:, :, :, :-1]
        return jnp.concatenate([prev, t], axis=-2)

    qb = to_streams(q)
    kc = with_prev(to_streams(k))
    vc = with_prev(to_streams(v))

    s = jnp.einsum('brhnqc,brhnkc->brhnqk', qb, kc).astype(jnp.float32) * (Dh ** -0.5)
    qi = jnp.arange(QBLK)[:, None] + QBLK
    kj = jnp.arange(2 * QBLK)[None, :]
    dist = qi - kj
    band = (dist >= 0) & (dist <= steps)
    blk = jnp.arange(nb)[:, None, None]
    mask = band[None] & ((blk > 0) | (kj[None] >= QBLK))
    s = jnp.where(mask, s, NEG)
    m = jnp.max(s, axis=-1, keepdims=True)
    p = jnp.exp(s - m)
    den = jnp.sum(p, axis=-1, keepdims=True)
    o = jnp.einsum('brhnqk,brhnkc->brhnqc', (p / den).astype(v.dtype), vc)
    lse = (m + jnp.log(den))[..., 0]

    o = o.reshape(Bsz, dilation, H, n, Dh).transpose(0, 3, 1, 2, 4).reshape(Bsz, L, H, Dh)[:, :S]
    lse = lse.reshape(Bsz, dilation, H, n).transpose(0, 3, 1, 2).reshape(Bsz, L, H)[:, :S]
    return o, lse


def causal_pool_minus_self(u):
    S = u.shape[1]
    cs = jnp.cumsum(u.astype(jnp.float32), axis=1)
    t = jnp.arange(S)
    outs = []
    for g, p in enumerate(POOL_SIZES):
        c = cs[:, :, g]
        prev = jnp.pad(c, ((0, 0), (p, 0), (0, 0)))[:, :S]
        cnt = jnp.minimum(t + 1, p).astype(jnp.float32)[None, :, None]
        outs.append((c - prev) / cnt - u[:, :, g].astype(jnp.float32))
    return jnp.stack(outs, axis=2).astype(u.dtype)


def even_layer(x, cos, sin, norm_w, w_in, q_norm_w, k_norm_w, conv_w, w_out):
    Bsz, S, _ = x.shape
    h = rms_norm(x, norm_w)
    proj = h @ w_in
    nA = N_GROUPS * A_WIDTH
    cuts = np.cumsum([nA, nA, nA, B_WIDTH, B_WIDTH, B_WIDTH]).tolist()
    q, k, v, bg, cg, hb, z = jnp.split(proj, cuts, axis=-1)
    shp = (Bsz, S, N_GROUPS, A_HEADS, HEAD_DIM)
    q = apply_rope(rms_norm(q.reshape(shp), q_norm_w), cos, sin)
    k = apply_rope(rms_norm(k.reshape(shp), k_norm_w), cos, sin)
    v = v.reshape(shp)
    outs, lses = [], []
    for g, (win, dil) in enumerate(A_GROUPS):
        o, l = dilated_attention(q[:, :, g], k[:, :, g], v[:, :, g], win, dil)
        outs.append(o)
        lses.append(l)
    wts = jax.nn.softmax(jnp.stack(lses, axis=0), axis=0)
    o_a = jnp.sum(wts[..., None] * jnp.stack(outs, axis=0).astype(jnp.float32), axis=0)
    o_a = o_a.astype(x.dtype).reshape(Bsz, S, A_WIDTH)
    y_b = bg * causal_dwconv(cg * hb, conv_w)
    u = jnp.concatenate([o_a, y_b], axis=-1) * jax.nn.silu(z)
    return x + u @ w_out


def odd_layer(x, norm_w, w_in, pool_w, pool_scale, dconv_w, dconv_b, ln_w, ln_b, w_out):
    Bsz, S, _ = x.shape
    h = rms_norm(x, norm_w)
    proj = h @ w_in
    cuts = np.cumsum([C_WIDTH, D_WIDTH, D_WIDTH]).tolist()
    uc, da, dg, z = jnp.split(proj, cuts, axis=-1)
    pooled = causal_pool_minus_self(uc.reshape(Bsz, S, len(POOL_SIZES), C_GROUP))
    y_c = jnp.einsum('bsgc,gcd->bsgd', pooled, pool_w).reshape(Bsz, S, C_WIDTH) * pool_scale
    gl = da * jax.nn.sigmoid(dg)
    c = causal_dwconv(gl, dconv_w) + dconv_b
    y_d = jax.nn.silu(layer_norm(c, ln_w, ln_b))
    u = jnp.concatenate([y_c, y_d], axis=-1) * jax.nn.silu(z)
    return x + u @ w_out


def setup_inputs(seed: int = 0) -> dict:
    key = jax.random.key(seed)
    ks = jax.random.split(key, 20)
    f32 = jnp.float32
    nrm = lambda k, shape, scale: jax.random.normal(k, shape, f32) * scale
    x = jax.random.normal(ks[0], (BATCH, SEQ, D_MODEL), f32)
    offset = jax.random.randint(ks[1], (BATCH, 1), 0, 4096, dtype=jnp.int32)
    positions = offset + jnp.arange(SEQ, dtype=jnp.int32)[None, :]
    return {
        "x": x,
        "positions": positions,
        "e_norm_w": 1.0 + nrm(ks[2], (N_EVEN, D_MODEL), 0.02),
        "e_w_in": nrm(ks[3], (N_EVEN, D_MODEL, EVEN_IN), D_MODEL ** -0.5),
        "e_q_norm_w": 1.0 + nrm(ks[4], (N_EVEN, HEAD_DIM), 0.02),
        "e_k_norm_w": 1.0 + nrm(ks[5], (N_EVEN, HEAD_DIM), 0.02),
        "e_conv_w": nrm(ks[6], (N_EVEN, SC_WIDTH, B_WIDTH), SC_WIDTH ** -0.5),
        "e_w_out": nrm(ks[7], (N_EVEN, GATE_EVEN, D_MODEL), GATE_EVEN ** -0.5),
        "o_norm_w": 1.0 + nrm(ks[8], (N_ODD, D_MODEL), 0.02),
        "o_w_in": nrm(ks[9], (N_ODD, D_MODEL, ODD_IN), D_MODEL ** -0.5),
        "o_pool_w": nrm(ks[10], (N_ODD, len(POOL_SIZES), C_GROUP, C_GROUP), C_GROUP ** -0.5),
        "o_pool_scale": 1.0 + nrm(ks[11], (N_ODD, C_WIDTH), 0.02),
        "o_dconv_w": nrm(ks[12], (N_ODD, D_CONV, D_WIDTH), D_CONV ** -0.5),
        "o_dconv_b": nrm(ks[13], (N_ODD, D_WIDTH), 0.01),
        "o_ln_w": 1.0 + nrm(ks[14], (N_ODD, D_WIDTH), 0.02),
        "o_ln_b": nrm(ks[15], (N_ODD, D_WIDTH), 0.01),
        "o_w_out": nrm(ks[16], (N_ODD, GATE_ODD, D_MODEL), GATE_ODD ** -0.5),
    }


def reference(x, positions, e_norm_w, e_w_in, e_q_norm_w, e_k_norm_w, e_conv_w, e_w_out,
              o_norm_w, o_w_in, o_pool_w, o_pool_scale, o_dconv_w, o_dconv_b, o_ln_w, o_ln_b,
              o_w_out):
    cos, sin = rope_tables(positions)
    for i in range(DEPTH):
        j = i // 2
        if i % 2 == 0:
            x = even_layer(x, cos, sin, e_norm_w[j], e_w_in[j], e_q_norm_w[j], e_k_norm_w[j],
                           e_conv_w[j], e_w_out[j])
        else:
            x = odd_layer(x, o_norm_w[j], o_w_in[j], o_pool_w[j], o_pool_scale[j], o_dconv_w[j],
                          o_dconv_b[j], o_ln_w[j], o_ln_b[j], o_w_out[j])
    return x
```

```python
import functools

import numpy as np
import jax
import jax.numpy as jnp
from jax import lax
from jax.experimental import pallas as pl
from jax.experimental.pallas import tpu as pltpu

F32 = jnp.float32
BF16 = jnp.bfloat16

HEAD_DIM = 64
A_GROUPS = ((128, 1), (512, 4), (2048, 16))
N_GROUPS = len(A_GROUPS)
SC_WIDTH = 3
POOL_SIZES = (2, 4, 8, 16)
D_CONV = 31
ROT_DIM = HEAD_DIM // 4
ROT_HALF = ROT_DIM // 2
ROPE_THETA = 500000.0
QBLK = 128
EPS = 1e-6
NEG = -1e30

LANES = 128
SLAB = LANES
HEADS_PER_SLAB = SLAB // HEAD_DIM
MXU_WIDTH = 256
VMEM_LIMIT_BYTES = 56 * 1024 * 1024

ROW_TILE = 512
SPAN = max(d for _, d in A_GROUPS) * QBLK
SC_HALO = 8
POOL_HALO = 16
DCONV_HALO = 32
DCONV_ROWS = 32


def _qk_lane_source():
    src = np.zeros((SLAB,), np.int32)
    h0, h1 = 0, HEAD_DIM
    src[0:8] = h0 + np.arange(0, 8)
    src[8:16] = h1 + np.arange(0, 8)
    src[16:40] = h0 + np.arange(16, 40)
    src[40:64] = h1 + np.arange(16, 40)
    src[64:72] = h0 + np.arange(8, 16)
    src[72:80] = h1 + np.arange(8, 16)
    src[80:104] = h0 + np.arange(40, 64)
    src[104:128] = h1 + np.arange(40, 64)
    assert sorted(src.tolist()) == list(range(SLAB))
    return src


_QK_SRC = _qk_lane_source()


def _band_bias(first):
    qi = np.arange(QBLK)[:, None] + QBLK
    kj = np.arange(2 * QBLK)[None, :]
    dist = qi - kj
    steps = A_GROUPS[0][0] // A_GROUPS[0][1]
    ok = (dist >= 0) & (dist <= steps)
    if first:
        ok = ok & (kj >= QBLK)
    return np.where(ok, 0.0, NEG).astype(np.float32)


def _silu(v):
    return v * jax.nn.sigmoid(v)


def _even_in_kernel(x_ref, pos_ref, nw_ref, win_ref, qkw_ref, hm_ref, invf_ref, sgn_ref, cw_ref,
                    a0_ref, a1_ref, a2_ref, ub_ref, gza_ref,
                    y_scr, conv_scr, *, tm, tiles_per_seq, a_width):
    ti = pl.program_id(0) % tiles_per_seq
    out_refs = (a0_ref, a1_ref, a2_ref)
    n_pairs = a_width // SLAB
    n_slabs = N_GROUPS * n_pairs

    x = x_ref[...]
    ms = jnp.mean(x * x, axis=-1, keepdims=True)
    h = (x * lax.rsqrt(ms + EPS) * nw_ref[...]).astype(BF16)

    ang = pos_ref[...].astype(F32) * invf_ref[...]
    rope_c = jnp.cos(ang)
    rope_s = jnp.sin(ang) * sgn_ref[...]
    hm0 = hm_ref[0:1, :]
    hm1 = hm_ref[1:2, :]

    def emit(slab_val, role, g, pair, buf):
        d = A_GROUPS[g][1]
        idx = role * n_pairs + pair
        if d == 1:
            out_refs[g][idx, 0, 0] = slab_val.astype(BF16)
            return
        y_scr[buf] = slab_val
        for r in range(d):
            out_refs[g][idx, 0, r] = y_scr[buf, pl.ds(r, tm // d, stride=d), :].astype(BF16)

    chunks_per_role = n_slabs * SLAB // MXU_WIDTH
    slabs_per_chunk = MXU_WIDTH // SLAB
    for role in range(3):
        for ch in range(chunks_per_role):
            col = (role * chunks_per_role + ch) * MXU_WIDTH
            res = jnp.dot(h, win_ref[:, col:col + MXU_WIDTH], preferred_element_type=F32)
            for s in range(slabs_per_chunk):
                slab = ch * slabs_per_chunk + s
                g, pair = divmod(slab, n_pairs)
                v = res[:, s * SLAB:(s + 1) * SLAB]
                if role < 2:
                    sq = v * v
                    ss0 = jnp.sum(sq * hm0, axis=-1, keepdims=True)
                    ss1 = jnp.sum(sq * hm1, axis=-1, keepdims=True)
                    inv = (lax.rsqrt(ss0 * (1.0 / HEAD_DIM) + EPS) * hm0
                           + lax.rsqrt(ss1 * (1.0 / HEAD_DIM) + EPS) * hm1)
                    vn = v * inv * qkw_ref[role:role + 1, :]
                    v = vn * rope_c + pltpu.roll(vn, SLAB // 2, axis=1) * rope_s
                emit(v, role, g, pair, (slab + role) % y_scr.shape[0])

    b_width = ub_ref.shape[-1]
    base = 3 * n_slabs * SLAB
    bg = jnp.dot(h, win_ref[:, base:base + b_width], preferred_element_type=F32)
    cg = jnp.dot(h, win_ref[:, base + b_width:base + 2 * b_width], preferred_element_type=F32)
    hb = jnp.dot(h, win_ref[:, base + 2 * b_width:base + 3 * b_width], preferred_element_type=F32)
    zbase = base + 3 * b_width
    za = jnp.dot(h, win_ref[:, zbase:zbase + a_width], preferred_element_type=F32)
    zb = jnp.dot(h, win_ref[:, zbase + a_width:zbase + a_width + b_width], preferred_element_type=F32)

    @pl.when(ti == 0)
    def _():
        conv_scr[0:SC_HALO, :] = jnp.zeros((SC_HALO, b_width), F32)

    @pl.when(ti != 0)
    def _():
        conv_scr[0:SC_HALO, :] = conv_scr[tm:tm + SC_HALO, :]

    conv_scr[SC_HALO:SC_HALO + tm, :] = cg * hb
    y = jnp.zeros((tm, b_width), F32)
    for k in range(SC_WIDTH):
        off = SC_HALO - (SC_WIDTH - 1) + k
        y = y + conv_scr[off:off + tm, :] * cw_ref[k:k + 1, :]
    ub_ref[...] = (bg * y * _silu(zb)).astype(BF16)
    gza_ref[...] = _silu(za).astype(BF16)


def _even_in(x2, pos2, nw, win_p, qkw, hm, invf, sgn, cw, *, batch, seq, a_width, b_width):
    n, dm = x2.shape
    tm = ROW_TILE
    tiles_per_seq = seq // tm
    n_pairs = a_width // SLAB
    ncols = win_p.shape[1]
    const = lambda i: (0, 0)
    a_shapes, a_specs = [], []
    for _, d in A_GROUPS:
        a_shapes.append(jax.ShapeDtypeStruct((3 * n_pairs, batch, d, seq // d, SLAB), BF16))
        a_specs.append(pl.BlockSpec((3 * n_pairs, 1, d, tm // d, SLAB),
                                    lambda i: (0, i // tiles_per_seq, 0, i % tiles_per_seq, 0)))
    row = lambda w: pl.BlockSpec((tm, w), lambda i: (i, 0))
    return pl.pallas_call(
        functools.partial(_even_in_kernel, tm=tm, tiles_per_seq=tiles_per_seq, a_width=a_width),
        grid=(n // tm,),
        in_specs=[row(dm), row(1),
                  pl.BlockSpec((1, dm), const),
                  pl.BlockSpec((dm, ncols), const, pipeline_mode=pl.Buffered(1)),
                  pl.BlockSpec((2, SLAB), const), pl.BlockSpec((2, SLAB), const),
                  pl.BlockSpec((1, SLAB), const), pl.BlockSpec((1, SLAB), const),
                  pl.BlockSpec((SC_WIDTH, b_width), const)],
        out_specs=a_specs + [row(b_width), row(a_width)],
        out_shape=a_shapes + [jax.ShapeDtypeStruct((n, b_width), BF16),
                              jax.ShapeDtypeStruct((n, a_width), BF16)],
        scratch_shapes=[pltpu.VMEM((4, tm, SLAB), F32),
                        pltpu.VMEM((SC_HALO + tm, b_width), F32)],
        compiler_params=pltpu.CompilerParams(dimension_semantics=("arbitrary",),
                                             vmem_limit_bytes=VMEM_LIMIT_BYTES),
        name="even_in",
    )(x2, pos2, nw, win_p, qkw, hm, invf, sgn, cw)


def _attn_kernel(*refs):
    (q0, kc0, kp0, vc0, vp0,
     q1, kc1, kp1, vc1, vp1,
     q2, kc2, kp2, vc2, vp2,
     gz_ref, bias_ref, hm_ref, o_ref, acc_o, acc_m, acc_l) = refs
    groups = ((q0, kc0, kp0, vc0, vp0), (q1, kc1, kp1, vc1, vp1), (q2, kc2, kp2, vc2, vp2))
    j = pl.program_id(2)
    first_idx = jnp.where(j == 0, 1, 0)
    lane = lax.broadcasted_iota(jnp.int32, (QBLK, SLAB), 1)
    lo_half = lane < HEAD_DIM
    hm = (hm_ref[0:1, :], hm_ref[1:2, :])

    def block(q, k, v, bias):
        parts = []
        for hh in range(HEADS_PER_SLAB):
            s = lax.dot_general(q * hm[hh], k, (((1,), (1,)), ((), ())),
                                preferred_element_type=F32) + bias
            m = jnp.max(s, axis=-1, keepdims=True)
            p = jnp.exp(s - m)
            l = jnp.sum(p, axis=-1, keepdims=True)
            o = jnp.dot(p.astype(BF16), v, preferred_element_type=F32)
            parts.append((o, m, l))
        (o0, m0, l0), (o1, m1, l1) = parts
        return (jnp.where(lo_half, o0, o1), jnp.where(lo_half, m0, m1), jnp.where(lo_half, l0, l1))

    def first_block(g, r):
        q_ref, kc_ref, kp_ref, vc_ref, vp_ref = groups[g]
        k = jnp.concatenate([kp_ref[r], kc_ref[r, 0:QBLK, :]], axis=0)
        v = jnp.concatenate([vp_ref[r], vc_ref[r, 0:QBLK, :]], axis=0)
        return block(q_ref[r, 0:QBLK, :], k, v, bias_ref[first_idx])

    def later_block(g, r, t):
        q_ref, kc_ref, _, vc_ref, _ = groups[g]
        k0 = pl.multiple_of((t - 1) * QBLK, QBLK)
        q0_ = pl.multiple_of(t * QBLK, QBLK)
        return block(q_ref[r, pl.ds(q0_, QBLK), :], kc_ref[r, pl.ds(k0, 2 * QBLK), :],
                     vc_ref[r, pl.ds(k0, 2 * QBLK), :], bias_ref[0])

    def rows(g, r, t):
        d = A_GROUPS[g][1]
        if d == 1:
            return pl.ds(pl.multiple_of(t * QBLK, QBLK), QBLK)
        return pl.ds(t * QBLK * d + r, QBLK, stride=d)

    def merge(g, r, t, oml, assign, final):
        o, m, l = oml
        sl = rows(g, r, t)
        if not assign:
            m_old = acc_m[sl, :]
            m_new = jnp.maximum(m_old, m)
            a = jnp.exp(m_old - m_new)
            b = jnp.exp(m - m_new)
            o = acc_o[sl, :] * a + o * b
            l = acc_l[sl, :] * a + l * b
            m = m_new
        if final:
            o_ref[sl, :] = (o / l * gz_ref[sl, :].astype(F32)).astype(BF16)
        else:
            acc_o[sl, :] = o
            acc_m[sl, :] = m
            acc_l[sl, :] = l

    order = sorted(range(N_GROUPS), key=lambda g: -A_GROUPS[g][1])
    assert A_GROUPS[order[-1]][1] == 1
    for n_done, g in enumerate(order):
        d = A_GROUPS[g][1]
        nblk = SPAN // d // QBLK
        assign, final = n_done == 0, n_done == N_GROUPS - 1

        def stream(r, carry, g=g, nblk=nblk, assign=assign, final=final):
            merge(g, r, 0, first_block(g, r), assign, final)
            if nblk > 1:
                def later(t, c):
                    merge(g, r, t, later_block(g, r, t), assign, final)
                    return c
                lax.fori_loop(1, nblk, later, 0)
            return carry

        if d == 1:
            stream(0, 0)
        else:
            lax.fori_loop(0, d, stream, 0)


def _attention(a_arrays, gza, bias, hm, *, batch, seq, a_width):
    n_pairs = a_width // SLAB
    spans = seq // SPAN
    in_specs, args = [], []
    for (_, d), arr in zip(A_GROUPS, a_arrays):
        nrow = SPAN // d
        blocks_per_span = nrow // QBLK

        def cur(role, d=d, nrow=nrow):
            return pl.BlockSpec((None, None, d, nrow, SLAB),
                                lambda b, c, j: (role * n_pairs + c, b, 0, j, 0))

        def prev(role, d=d, bps=blocks_per_span):
            return pl.BlockSpec((None, None, d, QBLK, SLAB),
                                lambda b, c, j: (role * n_pairs + c, b, 0, jnp.maximum(j * bps - 1, 0), 0))

        in_specs += [cur(0), cur(1), prev(1), cur(2), prev(2)]
        args += [arr] * 5
    slab_spec = pl.BlockSpec((SPAN, SLAB), lambda b, c, j: (b * spans + j, c))
    in_specs += [slab_spec,
                 pl.BlockSpec((2, QBLK, 2 * QBLK), lambda b, c, j: (0, 0, 0)),
                 pl.BlockSpec((2, SLAB), lambda b, c, j: (0, 0))]
    args += [gza, bias, hm]
    return pl.pallas_call(
        _attn_kernel,
        grid=(batch, n_pairs, spans),
        in_specs=in_specs,
        out_specs=slab_spec,
        out_shape=jax.ShapeDtypeStruct((batch * seq, a_width), BF16),
        scratch_shapes=[pltpu.VMEM((SPAN, SLAB), F32)] * 3,
        compiler_params=pltpu.CompilerParams(dimension_semantics=("arbitrary",) * 3,
                                             vmem_limit_bytes=VMEM_LIMIT_BYTES),
        name="attention",
    )(*args)


def _even_out_kernel(x_ref, ua_ref, ub_ref, wa_ref, wb_ref, o_ref):
    acc = jnp.dot(ua_ref[...], wa_ref[...], preferred_element_type=F32)
    acc = acc + jnp.dot(ub_ref[...], wb_ref[...], preferred_element_type=F32)
    o_ref[...] = x_ref[...] + acc


def _even_out(x2, ua, ub, wa, wb):
    n, dm = x2.shape
    tm = ROW_TILE
    const = lambda i: (0, 0)
    row = lambda w: pl.BlockSpec((tm, w), lambda i: (i, 0))
    return pl.pallas_call(
        _even_out_kernel,
        grid=(n // tm,),
        in_specs=[row(dm), row(ua.shape[1]), row(ub.shape[1]),
                  pl.BlockSpec(wa.shape, const), pl.BlockSpec(wb.shape, const)],
        out_specs=row(dm),
        out_shape=jax.ShapeDtypeStruct((n, dm), F32),
        compiler_params=pltpu.CompilerParams(dimension_semantics=("arbitrary",),
                                             vmem_limit_bytes=VMEM_LIMIT_BYTES),
        name="even_out",
    )(x2, ua, ub, wa, wb)


def _odd_kernel(x_ref, nw_ref, win_ref, pw_ref, ps_ref, dw_ref, db_ref, lnw_ref, lnb_ref, wout_ref,
                o_ref, pool_scr, conv_scr, u_scr, gd_scr, *, tm, tiles_per_seq, c_width, d_width):
    ti = pl.program_id(0) % tiles_per_seq
    c_group = c_width // len(POOL_SIZES)

    x = x_ref[...]
    ms = jnp.mean(x * x, axis=-1, keepdims=True)
    h = (x * lax.rsqrt(ms + EPS) * nw_ref[...]).astype(BF16)

    uc = jnp.dot(h, win_ref[:, 0:c_width], preferred_element_type=F32)
    da = jnp.dot(h, win_ref[:, c_width:c_width + d_width], preferred_element_type=F32)
    dg = jnp.dot(h, win_ref[:, c_width + d_width:c_width + 2 * d_width], preferred_element_type=F32)
    zoff = c_width + 2 * d_width
    zc = jnp.dot(h, win_ref[:, zoff:zoff + c_width], preferred_element_type=F32)

    @pl.when(ti == 0)
    def _():
        pool_scr[0:POOL_HALO, :] = jnp.zeros((POOL_HALO, c_width), F32)
        conv_scr[0:DCONV_HALO, :] = jnp.zeros((DCONV_HALO, d_width), F32)

    @pl.when(ti != 0)
    def _():
        pool_scr[0:POOL_HALO, :] = pool_scr[tm:tm + POOL_HALO, :]
        conv_scr[0:DCONV_HALO, :] = conv_scr[tm:tm + DCONV_HALO, :]

    pool_scr[POOL_HALO:POOL_HALO + tm, :] = uc
    conv_scr[DCONV_HALO:DCONV_HALO + tm, :] = da * jax.nn.sigmoid(dg)

    t_idx = ti * tm + lax.broadcasted_iota(jnp.int32, (tm, 1), 0)
    gate_c = _silu(zc)
    for g, p in enumerate(POOL_SIZES):
        lanes = slice(g * c_group, (g + 1) * c_group)
        tot = pool_scr[POOL_HALO:POOL_HALO + tm, lanes]
        for back in range(1, p):
            tot = tot + pool_scr[POOL_HALO - back:POOL_HALO - back + tm, lanes]
        cnt = jnp.minimum(t_idx + 1, p).astype(F32)
        pooled = tot / cnt - uc[:, lanes]
        yc = jnp.dot(pooled.astype(BF16), pw_ref[g], preferred_element_type=F32) * ps_ref[:, lanes]
        u_scr[:, lanes] = (yc * gate_c[:, lanes]).astype(BF16)

    def conv_chunk(ci):
        r0 = ci * DCONV_ROWS
        acc = jnp.broadcast_to(db_ref[...], (DCONV_ROWS, d_width))
        for k in range(D_CONV):
            off = DCONV_HALO - (D_CONV - 1) + k
            acc = acc + conv_scr[r0 + off:r0 + off + DCONV_ROWS, :] * dw_ref[k:k + 1, :]
        mu = jnp.mean(acc, axis=-1, keepdims=True)
        cen = acc - mu
        var = jnp.mean(cen * cen, axis=-1, keepdims=True)
        yd = _silu(cen * lax.rsqrt(var + EPS) * lnw_ref[...] + lnb_ref[...])
        u_scr[r0:r0 + DCONV_ROWS, c_width:c_width + d_width] = (
            yd * gd_scr[r0:r0 + DCONV_ROWS, :]).astype(BF16)

    zd = jnp.dot(h, win_ref[:, zoff + c_width:zoff + c_width + d_width], preferred_element_type=F32)
    gd_scr[...] = _silu(zd)
    for ci in range(tm // DCONV_ROWS):
        conv_chunk(ci)

    o_ref[...] = x + jnp.dot(u_scr[...], wout_ref[...], preferred_element_type=F32)


def _odd_layer(x2, nw, win, pw, ps, dw, db, lnw, lnb, wout, *, seq, c_width, d_width):
    n, dm = x2.shape
    tm = ROW_TILE
    const2 = lambda i: (0, 0)
    full = lambda a: pl.BlockSpec(a.shape, (lambda i: (0,) * a.ndim))
    row = pl.BlockSpec((tm, dm), lambda i: (i, 0))
    return pl.pallas_call(
        functools.partial(_odd_kernel, tm=tm, tiles_per_seq=seq // tm, c_width=c_width, d_width=d_width),
        grid=(n // tm,),
        in_specs=[row, full(nw), full(win), full(pw), full(ps), full(dw), full(db), full(lnw),
                  full(lnb), full(wout)],
        out_specs=row,
        out_shape=jax.ShapeDtypeStruct((n, dm), F32),
        scratch_shapes=[pltpu.VMEM((POOL_HALO + tm, c_width), F32),
                        pltpu.VMEM((DCONV_HALO + tm, d_width), F32),
                        pltpu.VMEM((tm, c_width + d_width), BF16),
                        pltpu.VMEM((tm, d_width), F32)],
        compiler_params=pltpu.CompilerParams(dimension_semantics=("arbitrary",),
                                             vmem_limit_bytes=VMEM_LIMIT_BYTES),
        name="odd_layer",
    )(x2, nw, win, pw, ps, dw, db, lnw, lnb, wout)


def _even_layer(x2, pos2, norm_w, w_in, q_norm_w, k_norm_w, conv_w, w_out, *, batch, seq):
    dm = x2.shape[1]
    a_width = dm // 2
    b_width = dm // 2
    n_a = N_GROUPS * a_width
    assert seq % SPAN == 0 and seq % ROW_TILE == 0 and ROW_TILE % (16 * max(d for _, d in A_GROUPS)) == 0
    assert all(w // d == QBLK for w, d in A_GROUPS)

    cols = np.arange(w_in.shape[1])
    for role in range(2):
        for s in range(n_a // SLAB):
            base = role * n_a + s * SLAB
            cols[base:base + SLAB] = base + _QK_SRC
    win_p = w_in[:, cols].astype(BF16)
    dim = _QK_SRC % HEAD_DIM
    qkw = jnp.stack([q_norm_w[dim] * (HEAD_DIM ** -0.5), k_norm_w[dim]]).astype(F32)
    head0 = (_QK_SRC // HEAD_DIM == 0)
    hm = jnp.asarray(np.stack([head0, ~head0]).astype(np.float32))
    inv_freq = ROPE_THETA ** (-jnp.arange(ROT_HALF, dtype=F32) / ROT_HALF)
    lane = np.arange(SLAB)
    rot = (lane % (SLAB // 2)) < ROT_DIM
    invf = jnp.where(jnp.asarray(rot), inv_freq[lane % ROT_HALF], 0.0).reshape(1, SLAB).astype(F32)
    sgn = jnp.asarray(np.where(lane < SLAB // 2, -1.0, 1.0).astype(np.float32)).reshape(1, SLAB)

    outs = _even_in(x2, pos2, norm_w.reshape(1, dm), win_p, qkw, hm, invf, sgn, conv_w,
                    batch=batch, seq=seq, a_width=a_width, b_width=b_width)
    a_arrays, ub, gza = outs[:N_GROUPS], outs[N_GROUPS], outs[N_GROUPS + 1]
    bias = jnp.asarray(np.stack([_band_bias(False), _band_bias(True)]))
    ua = _attention(a_arrays, gza, bias, hm.astype(BF16), batch=batch, seq=seq, a_width=a_width)
    return _even_out(x2, ua, ub, w_out[:a_width].astype(BF16), w_out[a_width:].astype(BF16))


def kernel(x, positions, e_norm_w, e_w_in, e_q_norm_w, e_k_norm_w, e_conv_w, e_w_out, o_norm_w, o_w_in, o_pool_w, o_pool_scale, o_dconv_w, o_dconv_b, o_ln_w, o_ln_b, o_w_out):
    batch, seq, dm = x.shape
    x2 = x.reshape(batch * seq, dm)
    pos2 = positions.reshape(batch * seq, 1)
    n_even, n_odd = e_norm_w.shape[0], o_norm_w.shape[0]
    for i in range(n_even + n_odd):
        j = i // 2
        if i % 2 == 0:
            x2 = _even_layer(x2, pos2, e_norm_w[j], e_w_in[j], e_q_norm_w[j], e_k_norm_w[j],
                             e_conv_w[j], e_w_out[j], batch=batch, seq=seq)
        else:
            c_width = d_width = dm // 2
            x2 = _odd_layer(x2, o_norm_w[j].reshape(1, dm), o_w_in[j].astype(BF16),
                            o_pool_w[j].astype(BF16), o_pool_scale[j].reshape(1, c_width),
                            o_dconv_w[j], o_dconv_b[j].reshape(1, d_width),
                            o_ln_w[j].reshape(1, d_width), o_ln_b[j].reshape(1, d_width),
                            o_w_out[j].astype(BF16), seq=seq, c_width=c_width, d_width=d_width)
    return x2.reshape(batch, seq, dm)
```

```python
import functools

import numpy as np
import jax
import jax.numpy as jnp
from jax import lax
from jax.experimental import pallas as pl
from jax.experimental.pallas import tpu as pltpu

F32 = jnp.float32
BF16 = jnp.bfloat16

HEAD_DIM = 64
A_GROUPS = ((128, 1), (512, 4), (2048, 16))
N_GROUPS = len(A_GROUPS)
SC_WIDTH = 3
POOL_SIZES = (2, 4, 8, 16)
D_CONV = 31
ROT_DIM = HEAD_DIM // 4
ROT_HALF = ROT_DIM // 2
ROPE_THETA = 500000.0
QBLK = 128
EPS = 1e-6
NEG = -1e30
LOG2E = 1.4426950408889634

LANES = 128
SLAB = LANES
HEADS_PER_SLAB = SLAB // HEAD_DIM
MXU_WIDTH = 256
VMEM_LIMIT_BYTES = 56 * 1024 * 1024

ROW_TILE = 512
SPAN = max(d for _, d in A_GROUPS) * QBLK
SC_HALO = 8
POOL_HALO = 16
DCONV_HALO = 32
DCONV_ROWS = 64
ATTN_UNROLL = 4


def _qk_lane_source():
    src = np.zeros((SLAB,), np.int32)
    h0, h1 = 0, HEAD_DIM
    src[0:8] = h0 + np.arange(0, 8)
    src[8:16] = h1 + np.arange(0, 8)
    src[16:40] = h0 + np.arange(16, 40)
    src[40:64] = h1 + np.arange(16, 40)
    src[64:72] = h0 + np.arange(8, 16)
    src[72:80] = h1 + np.arange(8, 16)
    src[80:104] = h0 + np.arange(40, 64)
    src[104:128] = h1 + np.arange(40, 64)
    assert sorted(src.tolist()) == list(range(SLAB))
    return src


_QK_SRC = _qk_lane_source()


def _band_bias(first):
    qi = np.arange(QBLK)[:, None] + QBLK
    kj = np.arange(2 * QBLK)[None, :]
    dist = qi - kj
    steps = A_GROUPS[0][0] // A_GROUPS[0][1]
    ok = (dist >= 0) & (dist <= steps)
    if first:
        ok = ok & (kj >= QBLK)
    return np.where(ok, 0.0, NEG).astype(np.float32)


def _silu(v):
    return v * jax.nn.sigmoid(v)


def _even_in_kernel(x_ref, pos_ref, nw_ref, win_ref, qkw_ref, hm_ref, invf_ref, sgn_ref, cw_ref,
                    a0_ref, a1_ref, a2_ref, ub_ref, gza_ref,
                    y_scr, conv_scr, *, tm, tiles_per_seq, a_width):
    ti = pl.program_id(0) % tiles_per_seq
    out_refs = (a0_ref, a1_ref, a2_ref)
    n_pairs = a_width // SLAB
    n_slabs = N_GROUPS * n_pairs

    x = x_ref[...]
    ms = jnp.mean(x * x, axis=-1, keepdims=True)
    h = (x * lax.rsqrt(ms + EPS) * nw_ref[...]).astype(BF16)

    ang = pos_ref[...].astype(F32) * invf_ref[...]
    rope_c = jnp.cos(ang)
    rope_s = jnp.sin(ang) * sgn_ref[...]
    hm0 = hm_ref[0:1, :]
    hm1 = hm_ref[1:2, :]

    def emit(slab_val, role, g, pair, buf):
        d = A_GROUPS[g][1]
        idx = role * n_pairs + pair
        if d == 1:
            out_refs[g][idx, 0, 0] = slab_val.astype(BF16)
            return
        y_scr[buf] = slab_val
        for r in range(d):
            out_refs[g][idx, 0, r] = y_scr[buf, pl.ds(r, tm // d, stride=d), :].astype(BF16)

    chunks_per_role = n_slabs * SLAB // MXU_WIDTH
    slabs_per_chunk = MXU_WIDTH // SLAB
    for role in range(3):
        for ch in range(chunks_per_role):
            col = (role * chunks_per_role + ch) * MXU_WIDTH
            res = jnp.dot(h, win_ref[:, col:col + MXU_WIDTH], preferred_element_type=F32)
            for s in range(slabs_per_chunk):
                slab = ch * slabs_per_chunk + s
                g, pair = divmod(slab, n_pairs)
                v = res[:, s * SLAB:(s + 1) * SLAB]
                if role < 2:
                    sq = v * v
                    ss0 = jnp.sum(sq * hm0, axis=-1, keepdims=True)
                    ss1 = jnp.sum(sq * hm1, axis=-1, keepdims=True)
                    inv = (lax.rsqrt(ss0 * (1.0 / HEAD_DIM) + EPS) * hm0
                           + lax.rsqrt(ss1 * (1.0 / HEAD_DIM) + EPS) * hm1)
                    vn = v * inv * qkw_ref[role:role + 1, :]
                    v = vn * rope_c + pltpu.roll(vn, SLAB // 2, axis=1) * rope_s
                emit(v, role, g, pair, (slab + role) % y_scr.shape[0])

    b_width = ub_ref.shape[-1]
    base = 3 * n_slabs * SLAB
    bg = jnp.dot(h, win_ref[:, base:base + b_width], preferred_element_type=F32)
    cg = jnp.dot(h, win_ref[:, base + b_width:base + 2 * b_width], preferred_element_type=F32)
    hb = jnp.dot(h, win_ref[:, base + 2 * b_width:base + 3 * b_width], preferred_element_type=F32)
    zbase = base + 3 * b_width
    za = jnp.dot(h, win_ref[:, zbase:zbase + a_width], preferred_element_type=F32)
    zb = jnp.dot(h, win_ref[:, zbase + a_width:zbase + a_width + b_width], preferred_element_type=F32)

    @pl.when(ti == 0)
    def _():
        conv_scr[0:SC_HALO, :] = jnp.zeros((SC_HALO, b_width), F32)

    @pl.when(ti != 0)
    def _():
        conv_scr[0:SC_HALO, :] = conv_scr[tm:tm + SC_HALO, :]

    conv_scr[SC_HALO:SC_HALO + tm, :] = cg * hb
    y = jnp.zeros((tm, b_width), F32)
    for k in range(SC_WIDTH):
        off = SC_HALO - (SC_WIDTH - 1) + k
        y = y + conv_scr[off:off + tm, :] * cw_ref[k:k + 1, :]
    ub_ref[...] = (bg * y * _silu(zb)).astype(BF16)
    gza_ref[...] = _silu(za).astype(BF16)


def _even_in(x2, pos2, nw, win_p, qkw, hm, invf, sgn, cw, *, batch, seq, a_width, b_width):
    n, dm = x2.shape
    tm = ROW_TILE
    tiles_per_seq = seq // tm
    n_pairs = a_width // SLAB
    ncols = win_p.shape[1]
    const = lambda i: (0, 0)
    a_shapes, a_specs = [], []
    for _, d in A_GROUPS:
        a_shapes.append(jax.ShapeDtypeStruct((3 * n_pairs, batch, d, seq // d, SLAB), BF16))
        a_specs.append(pl.BlockSpec((3 * n_pairs, 1, d, tm // d, SLAB),
                                    lambda i: (0, i // tiles_per_seq, 0, i % tiles_per_seq, 0)))
    row = lambda w: pl.BlockSpec((tm, w), lambda i: (i, 0))
    return pl.pallas_call(
        functools.partial(_even_in_kernel, tm=tm, tiles_per_seq=tiles_per_seq, a_width=a_width),
        grid=(n // tm,),
        in_specs=[row(dm), row(1),
                  pl.BlockSpec((1, dm), const),
                  pl.BlockSpec((dm, ncols), const, pipeline_mode=pl.Buffered(1)),
                  pl.BlockSpec((2, SLAB), const), pl.BlockSpec((2, SLAB), const),
                  pl.BlockSpec((1, SLAB), const), pl.BlockSpec((1, SLAB), const),
                  pl.BlockSpec((SC_WIDTH, b_width), const)],
        out_specs=a_specs + [row(b_width), row(a_width)],
        out_shape=a_shapes + [jax.ShapeDtypeStruct((n, b_width), BF16),
                              jax.ShapeDtypeStruct((n, a_width), BF16)],
        scratch_shapes=[pltpu.VMEM((4, tm, SLAB), F32),
                        pltpu.VMEM((SC_HALO + tm, b_width), F32)],
        compiler_params=pltpu.CompilerParams(dimension_semantics=("arbitrary",),
                                             vmem_limit_bytes=VMEM_LIMIT_BYTES),
        name="even_in",
    )(x2, pos2, nw, win_p, qkw, hm, invf, sgn, cw)


def _attn_kernel(*refs):
    (q0, kc0, kp0, vc0, vp0,
     q1, kc1, kp1, vc1, vp1,
     q2, kc2, kp2, vc2, vp2,
     gz_ref, bias_ref, hm_ref, o_ref, acc_o, acc_m, acc_l) = refs
    groups = ((q0, kc0, kp0, vc0, vp0), (q1, kc1, kp1, vc1, vp1), (q2, kc2, kp2, vc2, vp2))
    j = pl.program_id(2)
    first_idx = jnp.where(j == 0, 1, 0)
    lane = lax.broadcasted_iota(jnp.int32, (QBLK, SLAB), 1)
    lo_half = lane < HEAD_DIM
    hm = (hm_ref[0:1, :], hm_ref[1:2, :])

    def block(q, k, v, bias):
        parts = []
        for hh in range(HEADS_PER_SLAB):
            s = lax.dot_general(q * hm[hh], k, (((1,), (1,)), ((), ())),
                                preferred_element_type=F32) + bias
            m = jnp.max(s, axis=-1, keepdims=True)
            p = jnp.exp2(s - m)
            l = jnp.sum(p, axis=-1, keepdims=True)
            o = jnp.dot(p.astype(BF16), v, preferred_element_type=F32)
            parts.append((o, m, l))
        (o0, m0, l0), (o1, m1, l1) = parts
        return (jnp.where(lo_half, o0, o1), jnp.where(lo_half, m0, m1), jnp.where(lo_half, l0, l1))

    def stream_block(g, r, t):
        q_ref, kc_ref, kp_ref, vc_ref, vp_ref = groups[g]
        cur0 = pl.multiple_of(t * QBLK, QBLK)
        q = q_ref[r, pl.ds(cur0, QBLK), :]
        k_cur = kc_ref[r, pl.ds(cur0, QBLK), :]
        v_cur = vc_ref[r, pl.ds(cur0, QBLK), :]
        if isinstance(t, int):
            assert t == 0
            k_prev, v_prev, bias_idx = kp_ref[r], vp_ref[r], first_idx
        else:
            head = t == 0
            prev0 = pl.multiple_of(jnp.maximum(t - 1, 0) * QBLK, QBLK)
            k_prev = jnp.where(head, kp_ref[r], kc_ref[r, pl.ds(prev0, QBLK), :])
            v_prev = jnp.where(head, vp_ref[r], vc_ref[r, pl.ds(prev0, QBLK), :])
            bias_idx = jnp.where(head, first_idx, 0)
        k = jnp.concatenate([k_prev, k_cur], axis=0)
        v = jnp.concatenate([v_prev, v_cur], axis=0)
        return block(q, k, v, bias_ref[bias_idx])

    def rows(g, r, t):
        d = A_GROUPS[g][1]
        if d == 1:
            return pl.ds(pl.multiple_of(t * QBLK, QBLK), QBLK)
        return pl.ds(t * QBLK * d + r, QBLK, stride=d)

    def merge(g, r, t, oml, assign, final):
        o, m, l = oml
        sl = rows(g, r, t)
        if not assign:
            m_old = acc_m[sl, :]
            m_new = jnp.maximum(m_old, m)
            a = jnp.exp2(m_old - m_new)
            b = jnp.exp2(m - m_new)
            o = acc_o[sl, :] * a + o * b
            l = acc_l[sl, :] * a + l * b
            m = m_new
        if final:
            o_ref[sl, :] = (o / l * gz_ref[sl, :].astype(F32)).astype(BF16)
        else:
            acc_o[sl, :] = o
            acc_m[sl, :] = m
            acc_l[sl, :] = l

    order = sorted(range(N_GROUPS), key=lambda g: -A_GROUPS[g][1])
    assert A_GROUPS[order[-1]][1] == 1
    for n_done, g in enumerate(order):
        d = A_GROUPS[g][1]
        nblk = SPAN // d // QBLK
        assign, final = n_done == 0, n_done == N_GROUPS - 1

        def one(idx, carry, g=g, d=d, nblk=nblk, assign=assign, final=final):
            r = 0 if d == 1 else lax.shift_right_logical(idx, nblk.bit_length() - 1)
            t = 0 if nblk == 1 else lax.bitwise_and(idx, nblk - 1)
            merge(g, r, t, stream_block(g, r, t), assign, final)
            return carry

        assert nblk & (nblk - 1) == 0
        lax.fori_loop(0, d * nblk, one, 0, unroll=ATTN_UNROLL)


def _attention(a_arrays, gza, bias, hm, *, batch, seq, a_width):
    n_pairs = a_width // SLAB
    spans = seq // SPAN
    in_specs, args = [], []
    for (_, d), arr in zip(A_GROUPS, a_arrays):
        nrow = SPAN // d
        blocks_per_span = nrow // QBLK

        def cur(role, d=d, nrow=nrow):
            return pl.BlockSpec((None, None, d, nrow, SLAB),
                                lambda b, c, j: (role * n_pairs + c, b, 0, j, 0))

        def prev(role, d=d, bps=blocks_per_span):
            return pl.BlockSpec((None, None, d, QBLK, SLAB),
                                lambda b, c, j: (role * n_pairs + c, b, 0, jnp.maximum(j * bps - 1, 0), 0))

        in_specs += [cur(0), cur(1), prev(1), cur(2), prev(2)]
        args += [arr] * 5
    slab_spec = pl.BlockSpec((SPAN, SLAB), lambda b, c, j: (b * spans + j, c))
    in_specs += [slab_spec,
                 pl.BlockSpec((2, QBLK, 2 * QBLK), lambda b, c, j: (0, 0, 0)),
                 pl.BlockSpec((2, SLAB), lambda b, c, j: (0, 0))]
    args += [gza, bias, hm]
    return pl.pallas_call(
        _attn_kernel,
        grid=(batch, n_pairs, spans),
        in_specs=in_specs,
        out_specs=slab_spec,
        out_shape=jax.ShapeDtypeStruct((batch * seq, a_width), BF16),
        scratch_shapes=[pltpu.VMEM((SPAN, SLAB), F32)] * 3,
        compiler_params=pltpu.CompilerParams(dimension_semantics=("arbitrary",) * 3,
                                             vmem_limit_bytes=VMEM_LIMIT_BYTES),
        name="attention",
    )(*args)


def _even_out_kernel(x_ref, ua_ref, ub_ref, wa_ref, wb_ref, o_ref):
    acc = jnp.dot(ua_ref[...], wa_ref[...], preferred_element_type=F32)
    acc = acc + jnp.dot(ub_ref[...], wb_ref[...], preferred_element_type=F32)
    o_ref[...] = x_ref[...] + acc


def _even_out(x2, ua, ub, wa, wb):
    n, dm = x2.shape
    tm = ROW_TILE
    const = lambda i: (0, 0)
    row = lambda w: pl.BlockSpec((tm, w), lambda i: (i, 0))
    return pl.pallas_call(
        _even_out_kernel,
        grid=(n // tm,),
        in_specs=[row(dm), row(ua.shape[1]), row(ub.shape[1]),
                  pl.BlockSpec(wa.shape, const), pl.BlockSpec(wb.shape, const)],
        out_specs=row(dm),
        out_shape=jax.ShapeDtypeStruct((n, dm), F32),
        compiler_params=pltpu.CompilerParams(dimension_semantics=("arbitrary",),
                                             vmem_limit_bytes=VMEM_LIMIT_BYTES),
        name="even_out",
    )(x2, ua, ub, wa, wb)


def _odd_kernel(x_ref, nw_ref, win_ref, pw_ref, ps_ref, dw_ref, db_ref, lnw_ref, lnb_ref, wout_ref,
                o_ref, pool_scr, conv_scr, u_scr, yd_scr, *, tm, tiles_per_seq, c_width, d_width):
    ti = pl.program_id(0) % tiles_per_seq
    c_group = c_width // len(POOL_SIZES)

    x = x_ref[...]
    ms = jnp.mean(x * x, axis=-1, keepdims=True)
    h = (x * lax.rsqrt(ms + EPS) * nw_ref[...]).astype(BF16)

    uc = jnp.dot(h, win_ref[:, 0:c_width], preferred_element_type=F32)
    da = jnp.dot(h, win_ref[:, c_width:c_width + d_width], preferred_element_type=F32)
    dg = jnp.dot(h, win_ref[:, c_width + d_width:c_width + 2 * d_width], preferred_element_type=F32)
    zoff = c_width + 2 * d_width
    zc = jnp.dot(h, win_ref[:, zoff:zoff + c_width], preferred_element_type=F32)

    d_tiles = d_width // LANES

    @pl.when(ti == 0)
    def _():
        pool_scr[0:POOL_HALO, :] = jnp.zeros((POOL_HALO, c_width), F32)
        conv_scr[:, 0:DCONV_HALO, :] = jnp.zeros((d_tiles, DCONV_HALO, LANES), F32)

    @pl.when(ti != 0)
    def _():
        pool_scr[0:POOL_HALO, :] = pool_scr[tm:tm + POOL_HALO, :]
        conv_scr[:, 0:DCONV_HALO, :] = conv_scr[:, tm:tm + DCONV_HALO, :]

    pool_scr[POOL_HALO:POOL_HALO + tm, :] = uc
    glu = da * jax.nn.sigmoid(dg)
    for c in range(d_tiles):
        conv_scr[c, DCONV_HALO:DCONV_HALO + tm, :] = glu[:, c * LANES:(c + 1) * LANES]

    t_idx = ti * tm + lax.broadcasted_iota(jnp.int32, (tm, 1), 0)
    gate_c = _silu(zc)
    for g, p in enumerate(POOL_SIZES):
        lanes = slice(g * c_group, (g + 1) * c_group)
        tot = pool_scr[POOL_HALO:POOL_HALO + tm, lanes]
        for back in range(1, p):
            tot = tot + pool_scr[POOL_HALO - back:POOL_HALO - back + tm, lanes]
        cnt = jnp.minimum(t_idx + 1, p).astype(F32)
        pooled = tot / cnt - uc[:, lanes]
        yc = jnp.dot(pooled.astype(BF16), pw_ref[g], preferred_element_type=F32) * ps_ref[:, lanes]
        u_scr[:, lanes] = (yc * gate_c[:, lanes]).astype(BF16)

    half = DCONV_ROWS // 2

    def conv_half(r0, parity):
        accs = []
        for c in range(d_tiles):
            lanes = slice(c * LANES, (c + 1) * LANES)
            acc = jnp.broadcast_to(db_ref[:, lanes], (half, LANES))
            for k in range(D_CONV):
                off = r0 + parity + DCONV_HALO - (D_CONV - 1) + k
                acc = acc + conv_scr[c, pl.ds(off, half, stride=2), :] * dw_ref[k:k + 1, lanes]
            accs.append(acc)
        acc = jnp.concatenate(accs, axis=1)
        mu = jnp.mean(acc, axis=-1, keepdims=True)
        cen = acc - mu
        var = jnp.mean(cen * cen, axis=-1, keepdims=True)
        yd = _silu(cen * lax.rsqrt(var + EPS) * lnw_ref[...] + lnb_ref[...])
        for c in range(d_tiles):
            yd_scr[c, pl.ds(r0 + parity, half, stride=2), :] = yd[:, c * LANES:(c + 1) * LANES]

    for ci in range(tm // DCONV_ROWS):
        for parity in range(2):
            conv_half(ci * DCONV_ROWS, parity)

    zd = jnp.dot(h, win_ref[:, zoff + c_width:zoff + c_width + d_width], preferred_element_type=F32)
    gate_d = _silu(zd)
    for c in range(d_tiles):
        lanes = slice(c * LANES, (c + 1) * LANES)
        u_scr[:, c_width + c * LANES:c_width + (c + 1) * LANES] = (yd_scr[c] * gate_d[:, lanes]).astype(BF16)

    o_ref[...] = x + jnp.dot(u_scr[...], wout_ref[...], preferred_element_type=F32)


def _odd_layer(x2, nw, win, pw, ps, dw, db, lnw, lnb, wout, *, seq, c_width, d_width):
    n, dm = x2.shape
    tm = ROW_TILE
    const2 = lambda i: (0, 0)
    full = lambda a: pl.BlockSpec(a.shape, (lambda i: (0,) * a.ndim))
    row = pl.BlockSpec((tm, dm), lambda i: (i, 0))
    return pl.pallas_call(
        functools.partial(_odd_kernel, tm=tm, tiles_per_seq=seq // tm, c_width=c_width, d_width=d_width),
        grid=(n // tm,),
        in_specs=[row, full(nw), full(win), full(pw), full(ps), full(dw), full(db), full(lnw),
                  full(lnb), full(wout)],
        out_specs=row,
        out_shape=jax.ShapeDtypeStruct((n, dm), F32),
        scratch_shapes=[pltpu.VMEM((POOL_HALO + tm, c_width), F32),
                        pltpu.VMEM((d_width // LANES, DCONV_HALO + tm, LANES), F32),
                        pltpu.VMEM((tm, c_width + d_width), BF16),
                        pltpu.VMEM((d_width // LANES, tm, LANES), F32)],
        compiler_params=pltpu.CompilerParams(dimension_semantics=("arbitrary",),
                                             vmem_limit_bytes=VMEM_LIMIT_BYTES),
        name="odd_layer",
    )(x2, nw, win, pw, ps, dw, db, lnw, lnb, wout)


def _even_layer(x2, pos2, norm_w, w_in, q_norm_w, k_norm_w, conv_w, w_out, *, batch, seq):
    dm = x2.shape[1]
    a_width = dm // 2
    b_width = dm // 2
    n_a = N_GROUPS * a_width
    assert seq % SPAN == 0 and seq % ROW_TILE == 0 and ROW_TILE % (16 * max(d for _, d in A_GROUPS)) == 0
    assert all(w // d == QBLK for w, d in A_GROUPS)

    cols = np.arange(w_in.shape[1])
    for role in range(2):
        for s in range(n_a // SLAB):
            base = role * n_a + s * SLAB
            cols[base:base + SLAB] = base + _QK_SRC
    win_p = w_in[:, cols].astype(BF16)
    dim = _QK_SRC % HEAD_DIM
    qkw = jnp.stack([q_norm_w[dim] * (HEAD_DIM ** -0.5 * LOG2E), k_norm_w[dim]]).astype(F32)
    head0 = (_QK_SRC // HEAD_DIM == 0)
    hm = jnp.asarray(np.stack([head0, ~head0]).astype(np.float32))
    inv_freq = ROPE_THETA ** (-jnp.arange(ROT_HALF, dtype=F32) / ROT_HALF)
    lane = np.arange(SLAB)
    rot = (lane % (SLAB // 2)) < ROT_DIM
    invf = jnp.where(jnp.asarray(rot), inv_freq[lane % ROT_HALF], 0.0).reshape(1, SLAB).astype(F32)
    sgn = jnp.asarray(np.where(lane < SLAB // 2, -1.0, 1.0).astype(np.float32)).reshape(1, SLAB)

    outs = _even_in(x2, pos2, norm_w.reshape(1, dm), win_p, qkw, hm, invf, sgn, conv_w,
                    batch=batch, seq=seq, a_width=a_width, b_width=b_width)
    a_arrays, ub, gza = outs[:N_GROUPS], outs[N_GROUPS], outs[N_GROUPS + 1]
    bias = jnp.asarray(np.stack([_band_bias(False), _band_bias(True)]))
    ua = _attention(a_arrays, gza, bias, hm.astype(BF16), batch=batch, seq=seq, a_width=a_width)
    return _even_out(x2, ua, ub, w_out[:a_width].astype(BF16), w_out[a_width:].astype(BF16))


def kernel(x, positions, e_norm_w, e_w_in, e_q_norm_w, e_k_norm_w, e_conv_w, e_w_out, o_norm_w, o_w_in, o_pool_w, o_pool_scale, o_dconv_w, o_dconv_b, o_ln_w, o_ln_b, o_w_out):
    batch, seq, dm = x.shape
    x2 = x.reshape(batch * seq, dm)
    pos2 = positions.reshape(batch * seq, 1)
    n_even, n_odd = e_norm_w.shape[0], o_norm_w.shape[0]
    for i in range(n_even + n_odd):
        j = i // 2
        if i % 2 == 0:
            x2 = _even_layer(x2, pos2, e_norm_w[j], e_w_in[j], e_q_norm_w[j], e_k_norm_w[j],
                             e_conv_w[j], e_w_out[j], batch=batch, seq=seq)
        else:
            c_width = d_width = dm // 2
            x2 = _odd_layer(x2, o_norm_w[j].reshape(1, dm), o_w_in[j].astype(BF16),
                            o_pool_w[j].astype(BF16), o_pool_scale[j].reshape(1, c_width),
                            o_dconv_w[j], o_dconv_b[j].reshape(1, d_width),
                            o_ln_w[j].reshape(1, d_width), o_ln_b[j].reshape(1, d_width),
                            o_w_out[j].astype(BF16), seq=seq, c_width=c_width, d_width=d_width)
    return x2.reshape(batch, seq, dm)
```

```python
import functools

import numpy as np
import jax
import jax.numpy as jnp
from jax import lax
from jax.experimental import pallas as pl
from jax.experimental.pallas import tpu as pltpu

F32 = jnp.float32
BF16 = jnp.bfloat16

HEAD_DIM = 64
A_GROUPS = ((128, 1), (512, 4), (2048, 16))
N_GROUPS = len(A_GROUPS)
SC_WIDTH = 3
POOL_SIZES = (2, 4, 8, 16)
D_CONV = 31
ROT_DIM = HEAD_DIM // 4
ROT_HALF = ROT_DIM // 2
ROPE_THETA = 500000.0
QBLK = 128
EPS = 1e-6
NEG = -1e30
LOG2E = 1.4426950408889634

LANES = 128
SLAB = LANES
HEADS_PER_SLAB = SLAB // HEAD_DIM
MXU_WIDTH = 256
VMEM_LIMIT_BYTES = 56 * 1024 * 1024

ROW_TILE = 512
SPAN = max(d for _, d in A_GROUPS) * QBLK
SC_HALO = 8
POOL_HALO = 16
DCONV_HALO = 32
DCONV_ROWS = 64
ATTN_UNROLL = 8


def _qk_lane_source():
    src = np.zeros((SLAB,), np.int32)
    h0, h1 = 0, HEAD_DIM
    src[0:8] = h0 + np.arange(0, 8)
    src[8:16] = h1 + np.arange(0, 8)
    src[16:40] = h0 + np.arange(16, 40)
    src[40:64] = h1 + np.arange(16, 40)
    src[64:72] = h0 + np.arange(8, 16)
    src[72:80] = h1 + np.arange(8, 16)
    src[80:104] = h0 + np.arange(40, 64)
    src[104:128] = h1 + np.arange(40, 64)
    assert sorted(src.tolist()) == list(range(SLAB))
    return src


_QK_SRC = _qk_lane_source()


def _band_bias(first):
    qi = np.arange(QBLK)[:, None] + QBLK
    kj = np.arange(2 * QBLK)[None, :]
    dist = qi - kj
    steps = A_GROUPS[0][0] // A_GROUPS[0][1]
    ok = (dist >= 0) & (dist <= steps)
    if first:
        ok = ok & (kj >= QBLK)
    return np.where(ok, 0.0, NEG).astype(np.float32)


def _silu(v):
    return v * jax.nn.sigmoid(v)


def _even_in_kernel(x_ref, pos_ref, nw_ref, win_ref, qkw_ref, hm_ref, invf_ref, cw_ref,
                    a0_ref, a1_ref, a2_ref, ub_ref, gza_ref,
                    y_scr, conv_scr, rope_scr, *, tm, tiles_per_seq, a_width):
    ti = pl.program_id(0) % tiles_per_seq
    out_refs = (a0_ref, a1_ref, a2_ref)
    n_pairs = a_width // SLAB
    n_slabs = N_GROUPS * n_pairs

    x = x_ref[...]
    ms = jnp.mean(x * x, axis=-1, keepdims=True)
    h = (x * lax.rsqrt(ms + EPS) * nw_ref[...]).astype(BF16)

    pos = pos_ref[0].astype(F32)
    ones = jnp.ones((ROT_HALF, LANES), F32)
    zeros = jnp.zeros((ROT_HALF, LANES), F32)
    pad = SLAB // 2 // ROT_HALF - 2
    for i in range(tm // LANES):
        ang = pos[i:i + 1, :] * invf_ref[...]
        c8, s8 = jnp.cos(ang), jnp.sin(ang)
        ct = jnp.concatenate(([c8, c8] + [ones] * pad) * 2, axis=0)
        st = jnp.concatenate([-s8, -s8] + [zeros] * pad + [s8, s8] + [zeros] * pad, axis=0)
        rope_scr[0, i * LANES:(i + 1) * LANES, :] = ct.T
        rope_scr[1, i * LANES:(i + 1) * LANES, :] = st.T
    hm0 = hm_ref[0:1, :]
    hm1 = hm_ref[1:2, :]

    def emit(slab_val, role, g, pair, buf):
        d = A_GROUPS[g][1]
        idx = role * n_pairs + pair
        if d == 1:
            out_refs[g][idx, 0, 0] = slab_val.astype(BF16)
            return
        y_scr[buf] = slab_val
        for r in range(d):
            out_refs[g][idx, 0, r] = y_scr[buf, pl.ds(r, tm // d, stride=d), :].astype(BF16)

    b_width = ub_ref.shape[-1]
    base = 3 * n_slabs * SLAB
    zbase = base + 3 * b_width

    def proj(lo, width):
        return jnp.dot(h, win_ref[:, lo:lo + width], preferred_element_type=F32)

    def gates():
        bg = proj(base, b_width)
        cg = proj(base + b_width, b_width)
        hb = proj(base + 2 * b_width, b_width)
        za = proj(zbase, a_width)
        zb = proj(zbase + a_width, b_width)

        @pl.when(ti == 0)
        def _():
            conv_scr[0:SC_HALO, :] = jnp.zeros((SC_HALO, b_width), F32)

        @pl.when(ti != 0)
        def _():
            conv_scr[0:SC_HALO, :] = conv_scr[tm:tm + SC_HALO, :]

        conv_scr[SC_HALO:SC_HALO + tm, :] = cg * hb
        y = jnp.zeros((tm, b_width), F32)
        for k in range(SC_WIDTH):
            off = SC_HALO - (SC_WIDTH - 1) + k
            y = y + conv_scr[off:off + tm, :] * cw_ref[k:k + 1, :]
        ub_ref[...] = (bg * y * _silu(zb)).astype(BF16)
        gza_ref[...] = _silu(za).astype(BF16)

    def qkv():
        chunks_per_role = n_slabs * SLAB // MXU_WIDTH
        slabs_per_chunk = MXU_WIDTH // SLAB
        for role in range(3):
            for ch in range(chunks_per_role):
                col = (role * chunks_per_role + ch) * MXU_WIDTH
                res = proj(col, MXU_WIDTH)
                for s in range(slabs_per_chunk):
                    slab = ch * slabs_per_chunk + s
                    g, pair = divmod(slab, n_pairs)
                    v = res[:, s * SLAB:(s + 1) * SLAB]
                    if role < 2:
                        sq = v * v
                        ss0 = jnp.sum(sq * hm0, axis=-1, keepdims=True)
                        ss1 = jnp.sum(sq * hm1, axis=-1, keepdims=True)
                        inv = (lax.rsqrt(ss0 * (1.0 / HEAD_DIM) + EPS) * hm0
                               + lax.rsqrt(ss1 * (1.0 / HEAD_DIM) + EPS) * hm1)
                        vn = v * inv * qkw_ref[role:role + 1, :]
                        v = vn * rope_scr[0] + pltpu.roll(vn, SLAB // 2, axis=1) * rope_scr[1]
                    emit(v, role, g, pair, (slab + role) % y_scr.shape[0])

    qkv()
    gates()


def _even_in(x2, pos3, nw, win_p, qkw, hm, invf, cw, *, batch, seq, a_width, b_width):
    n, dm = x2.shape
    tm = ROW_TILE
    tiles_per_seq = seq // tm
    n_pairs = a_width // SLAB
    ncols = win_p.shape[1]
    const = lambda i: (0, 0)
    a_shapes, a_specs = [], []
    for _, d in A_GROUPS:
        a_shapes.append(jax.ShapeDtypeStruct((3 * n_pairs, batch, d, seq // d, SLAB), BF16))
        a_specs.append(pl.BlockSpec((3 * n_pairs, 1, d, tm // d, SLAB),
                                    lambda i: (0, i // tiles_per_seq, 0, i % tiles_per_seq, 0)))
    row = lambda w: pl.BlockSpec((tm, w), lambda i: (i, 0))
    return pl.pallas_call(
        functools.partial(_even_in_kernel, tm=tm, tiles_per_seq=tiles_per_seq, a_width=a_width),
        grid=(n // tm,),
        in_specs=[row(dm), pl.BlockSpec((1, tm // LANES, LANES), lambda i: (i, 0, 0)),
                  pl.BlockSpec((1, dm), const),
                  pl.BlockSpec((dm, ncols), const, pipeline_mode=pl.Buffered(1)),
                  pl.BlockSpec((2, SLAB), const), pl.BlockSpec((2, SLAB), const),
                  pl.BlockSpec((ROT_HALF, LANES), const),
                  pl.BlockSpec((SC_WIDTH, b_width), const)],
        out_specs=a_specs + [row(b_width), row(a_width)],
        out_shape=a_shapes + [jax.ShapeDtypeStruct((n, b_width), BF16),
                              jax.ShapeDtypeStruct((n, a_width), BF16)],
        scratch_shapes=[pltpu.VMEM((4, tm, SLAB), F32),
                        pltpu.VMEM((SC_HALO + tm, b_width), F32),
                        pltpu.VMEM((2, tm, SLAB), F32)],
        compiler_params=pltpu.CompilerParams(dimension_semantics=("arbitrary",),
                                             vmem_limit_bytes=VMEM_LIMIT_BYTES),
        name="even_in",
    )(x2, pos3, nw, win_p, qkw, hm, invf, cw)


def _attn_kernel(*refs):
    (q0, kc0, kp0, vc0, vp0,
     q1, kc1, kp1, vc1, vp1,
     q2, kc2, kp2, vc2, vp2,
     gz_ref, bias_ref, hm_ref, o_ref, acc_o, acc_m, acc_l) = refs
    groups = ((q0, kc0, kp0, vc0, vp0), (q1, kc1, kp1, vc1, vp1), (q2, kc2, kp2, vc2, vp2))
    j = pl.program_id(2)
    first_idx = jnp.where(j == 0, 1, 0)
    lane = lax.broadcasted_iota(jnp.int32, (QBLK, SLAB), 1)
    lo_half = lane < HEAD_DIM
    hm = (hm_ref[0:1, :], hm_ref[1:2, :])

    def block(q, k, v, bias):
        parts = []
        for hh in range(HEADS_PER_SLAB):
            s = lax.dot_general(q * hm[hh], k, (((1,), (1,)), ((), ())),
                                preferred_element_type=F32) + bias
            m = jnp.max(s, axis=-1, keepdims=True)
            p = jnp.exp2(s - m)
            l = jnp.sum(p, axis=-1, keepdims=True)
            o = jnp.dot(p.astype(BF16), v, preferred_element_type=F32)
            parts.append((o, m, l))
        (o0, m0, l0), (o1, m1, l1) = parts
        return (jnp.where(lo_half, o0, o1), jnp.where(lo_half, m0, m1), jnp.where(lo_half, l0, l1))

    def stream_block(g, r, t):
        q_ref, kc_ref, kp_ref, vc_ref, vp_ref = groups[g]
        cur0 = pl.multiple_of(t * QBLK, QBLK)
        q = q_ref[r, pl.ds(cur0, QBLK), :]
        k_cur = kc_ref[r, pl.ds(cur0, QBLK), :]
        v_cur = vc_ref[r, pl.ds(cur0, QBLK), :]
        if isinstance(t, int):
            assert t == 0
            k_prev, v_prev, bias_idx = kp_ref[r], vp_ref[r], first_idx
        else:
            head = t == 0
            prev0 = pl.multiple_of(jnp.maximum(t - 1, 0) * QBLK, QBLK)
            k_prev = jnp.where(head, kp_ref[r], kc_ref[r, pl.ds(prev0, QBLK), :])
            v_prev = jnp.where(head, vp_ref[r], vc_ref[r, pl.ds(prev0, QBLK), :])
            bias_idx = jnp.where(head, first_idx, 0)
        k = jnp.concatenate([k_prev, k_cur], axis=0)
        v = jnp.concatenate([v_prev, v_cur], axis=0)
        return block(q, k, v, bias_ref[bias_idx])

    def rows(g, r, t):
        d = A_GROUPS[g][1]
        if d == 1:
            return pl.ds(pl.multiple_of(t * QBLK, QBLK), QBLK)
        return pl.ds(t * QBLK * d + r, QBLK, stride=d)

    def merge(g, r, t, oml, assign, final):
        o, m, l = oml
        sl = rows(g, r, t)
        if not assign:
            m_old = acc_m[sl, :]
            m_new = jnp.maximum(m_old, m)
            a = jnp.exp2(m_old - m_new)
            b = jnp.exp2(m - m_new)
            o = acc_o[sl, :] * a + o * b
            l = acc_l[sl, :] * a + l * b
            m = m_new
        if final:
            o_ref[sl, :] = (o / l * gz_ref[sl, :].astype(F32)).astype(BF16)
        else:
            acc_o[sl, :] = o
            acc_m[sl, :] = m
            acc_l[sl, :] = l

    order = sorted(range(N_GROUPS), key=lambda g: -A_GROUPS[g][1])
    assert A_GROUPS[order[-1]][1] == 1
    for n_done, g in enumerate(order):
        d = A_GROUPS[g][1]
        nblk = SPAN // d // QBLK
        assign, final = n_done == 0, n_done == N_GROUPS - 1

        def one(idx, carry, g=g, d=d, nblk=nblk, assign=assign, final=final):
            r = 0 if d == 1 else lax.shift_right_logical(idx, nblk.bit_length() - 1)
            t = 0 if nblk == 1 else lax.bitwise_and(idx, nblk - 1)
            merge(g, r, t, stream_block(g, r, t), assign, final)
            return carry

        assert nblk & (nblk - 1) == 0
        lax.fori_loop(0, d * nblk, one, 0, unroll=ATTN_UNROLL)


def _attention(a_arrays, gza, bias, hm, *, batch, seq, a_width):
    n_pairs = a_width // SLAB
    spans = seq // SPAN
    in_specs, args = [], []
    for (_, d), arr in zip(A_GROUPS, a_arrays):
        nrow = SPAN // d
        blocks_per_span = nrow // QBLK

        def cur(role, d=d, nrow=nrow):
            return pl.BlockSpec((None, None, d, nrow, SLAB),
                                lambda b, c, j: (role * n_pairs + c, b, 0, j, 0))

        def prev(role, d=d, bps=blocks_per_span):
            return pl.BlockSpec((None, None, d, QBLK, SLAB),
                                lambda b, c, j: (role * n_pairs + c, b, 0, jnp.maximum(j * bps - 1, 0), 0))

        in_specs += [cur(0), cur(1), prev(1), cur(2), prev(2)]
        args += [arr] * 5
    slab_spec = pl.BlockSpec((SPAN, SLAB), lambda b, c, j: (b * spans + j, c))
    in_specs += [slab_spec,
                 pl.BlockSpec((2, QBLK, 2 * QBLK), lambda b, c, j: (0, 0, 0)),
                 pl.BlockSpec((2, SLAB), lambda b, c, j: (0, 0))]
    args += [gza, bias, hm]
    return pl.pallas_call(
        _attn_kernel,
        grid=(batch, n_pairs, spans),
        in_specs=in_specs,
        out_specs=slab_spec,
        out_shape=jax.ShapeDtypeStruct((batch * seq, a_width), BF16),
        scratch_shapes=[pltpu.VMEM((SPAN, SLAB), F32)] * 3,
        compiler_params=pltpu.CompilerParams(dimension_semantics=("arbitrary",) * 3,
                                             vmem_limit_bytes=VMEM_LIMIT_BYTES),
        name="attention",
    )(*args)


def _even_out_kernel(x_ref, ua_ref, ub_ref, wa_ref, wb_ref, o_ref):
    acc = jnp.dot(ua_ref[...], wa_ref[...], preferred_element_type=F32)
    acc = acc + jnp.dot(ub_ref[...], wb_ref[...], preferred_element_type=F32)
    o_ref[...] = x_ref[...] + acc


def _even_out(x2, ua, ub, wa, wb):
    n, dm = x2.shape
    tm = ROW_TILE
    const = lambda i: (0, 0)
    row = lambda w: pl.BlockSpec((tm, w), lambda i: (i, 0))
    return pl.pallas_call(
        _even_out_kernel,
        grid=(n // tm,),
        in_specs=[row(dm), row(ua.shape[1]), row(ub.shape[1]),
                  pl.BlockSpec(wa.shape, const), pl.BlockSpec(wb.shape, const)],
        out_specs=row(dm),
        out_shape=jax.ShapeDtypeStruct((n, dm), F32),
        compiler_params=pltpu.CompilerParams(dimension_semantics=("arbitrary",),
                                             vmem_limit_bytes=VMEM_LIMIT_BYTES),
        name="even_out",
    )(x2, ua, ub, wa, wb)


def _odd_kernel(x_ref, nw_ref, win_ref, pw_ref, ps_ref, dw_ref, db_ref, lnw_ref, lnb_ref, wout_ref,
                o_ref, pool_scr, conv_scr, u_scr, yd_scr, h_scr, *, tm, tiles_per_seq, c_width, d_width):
    ti = pl.program_id(0) % tiles_per_seq
    c_group = c_width // len(POOL_SIZES)
    d_tiles = d_width // LANES

    @pl.when(ti == 0)
    def _():
        pool_scr[0:POOL_HALO, :] = jnp.zeros((POOL_HALO, c_width), F32)
        conv_scr[:, 0:DCONV_HALO, :] = jnp.zeros((d_tiles, DCONV_HALO, LANES), F32)

    @pl.when(ti != 0)
    def _():
        pool_scr[0:POOL_HALO, :] = pool_scr[tm:tm + POOL_HALO, :]
        conv_scr[:, 0:DCONV_HALO, :] = conv_scr[:, tm:tm + DCONV_HALO, :]

    x = x_ref[...]
    ms = jnp.mean(x * x, axis=-1, keepdims=True)
    h_scr[...] = (x * lax.rsqrt(ms + EPS) * nw_ref[...]).astype(BF16)

    def proj(lo, width):
        return jnp.dot(h_scr[...], win_ref[:, lo:lo + width], preferred_element_type=F32)

    zoff = c_width + 2 * d_width
    glu = proj(c_width, d_width) * jax.nn.sigmoid(proj(c_width + d_width, d_width))
    for c in range(d_tiles):
        conv_scr[c, DCONV_HALO:DCONV_HALO + tm, :] = glu[:, c * LANES:(c + 1) * LANES]
    uc = proj(0, c_width)
    pool_scr[POOL_HALO:POOL_HALO + tm, :] = uc
    zc = proj(zoff, c_width)

    t_idx = ti * tm + lax.broadcasted_iota(jnp.int32, (tm, 1), 0)
    gate_c = _silu(zc)
    for g, p in enumerate(POOL_SIZES):
        lanes = slice(g * c_group, (g + 1) * c_group)
        tot = pool_scr[POOL_HALO:POOL_HALO + tm, lanes]
        for back in range(1, p):
            tot = tot + pool_scr[POOL_HALO - back:POOL_HALO - back + tm, lanes]
        cnt = jnp.minimum(t_idx + 1, p).astype(F32)
        pooled = tot / cnt - uc[:, lanes]
        yc = jnp.dot(pooled.astype(BF16), pw_ref[g], preferred_element_type=F32) * ps_ref[:, lanes]
        u_scr[:, lanes] = (yc * gate_c[:, lanes]).astype(BF16)

    half = DCONV_ROWS // 2

    def conv_half(r0, parity):
        accs = []
        for c in range(d_tiles):
            lanes = slice(c * LANES, (c + 1) * LANES)
            acc = jnp.broadcast_to(db_ref[:, lanes], (half, LANES))
            for k in range(D_CONV):
                off = r0 + parity + DCONV_HALO - (D_CONV - 1) + k
                acc = acc + conv_scr[c, pl.ds(off, half, stride=2), :] * dw_ref[k:k + 1, lanes]
            accs.append(acc)
        acc = jnp.concatenate(accs, axis=1)
        mu = jnp.mean(acc, axis=-1, keepdims=True)
        cen = acc - mu
        var = jnp.mean(cen * cen, axis=-1, keepdims=True)
        yd = _silu(cen * lax.rsqrt(var + EPS) * lnw_ref[...] + lnb_ref[...])
        for c in range(d_tiles):
            yd_scr[c, pl.ds(r0 + parity, half, stride=2), :] = yd[:, c * LANES:(c + 1) * LANES]

    for ci in range(tm // DCONV_ROWS):
        for parity in range(2):
            conv_half(ci * DCONV_ROWS, parity)

    zd = proj(zoff + c_width, d_width)
    gate_d = _silu(zd)
    for c in range(d_tiles):
        lanes = slice(c * LANES, (c + 1) * LANES)
        u_scr[:, c_width + c * LANES:c_width + (c + 1) * LANES] = (yd_scr[c] * gate_d[:, lanes]).astype(BF16)

    o_ref[...] = x + jnp.dot(u_scr[...], wout_ref[...], preferred_element_type=F32)


def _odd_layer(x2, nw, win, pw, ps, dw, db, lnw, lnb, wout, *, seq, c_width, d_width):
    n, dm = x2.shape
    tm = ROW_TILE
    const2 = lambda i: (0, 0)
    full = lambda a: pl.BlockSpec(a.shape, (lambda i: (0,) * a.ndim))
    row = pl.BlockSpec((tm, dm), lambda i: (i, 0))
    return pl.pallas_call(
        functools.partial(_odd_kernel, tm=tm, tiles_per_seq=seq // tm, c_width=c_width, d_width=d_width),
        grid=(n // tm,),
        in_specs=[row, full(nw), full(win), full(pw), full(ps), full(dw), full(db), full(lnw),
                  full(lnb), full(wout)],
        out_specs=row,
        out_shape=jax.ShapeDtypeStruct((n, dm), F32),
        scratch_shapes=[pltpu.VMEM((POOL_HALO + tm, c_width), F32),
                        pltpu.VMEM((d_width // LANES, DCONV_HALO + tm, LANES), F32),
                        pltpu.VMEM((tm, c_width + d_width), BF16),
                        pltpu.VMEM((d_width // LANES, tm, LANES), F32),
                        pltpu.VMEM((tm, dm), BF16)],
        compiler_params=pltpu.CompilerParams(dimension_semantics=("arbitrary",),
                                             vmem_limit_bytes=VMEM_LIMIT_BYTES),
        name="odd_layer",
    )(x2, nw, win, pw, ps, dw, db, lnw, lnb, wout)


def _even_layer(x2, pos3, norm_w, w_in, q_norm_w, k_norm_w, conv_w, w_out, *, batch, seq):
    dm = x2.shape[1]
    a_width = dm // 2
    b_width = dm // 2
    n_a = N_GROUPS * a_width
    assert seq % SPAN == 0 and seq % ROW_TILE == 0 and ROW_TILE % (16 * max(d for _, d in A_GROUPS)) == 0
    assert all(w // d == QBLK for w, d in A_GROUPS)

    cols = np.arange(w_in.shape[1])
    for role in range(2):
        for s in range(n_a // SLAB):
            base = role * n_a + s * SLAB
            cols[base:base + SLAB] = base + _QK_SRC
    win_p = w_in[:, cols].astype(BF16)
    dim = _QK_SRC % HEAD_DIM
    qkw = jnp.stack([q_norm_w[dim] * (HEAD_DIM ** -0.5 * LOG2E), k_norm_w[dim]]).astype(F32)
    head0 = (_QK_SRC // HEAD_DIM == 0)
    hm = jnp.asarray(np.stack([head0, ~head0]).astype(np.float32))
    inv_freq = ROPE_THETA ** (-jnp.arange(ROT_HALF, dtype=F32) / ROT_HALF)
    invf = jnp.broadcast_to(inv_freq[:, None], (ROT_HALF, LANES))

    outs = _even_in(x2, pos3, norm_w.reshape(1, dm), win_p, qkw, hm, invf, conv_w,
                    batch=batch, seq=seq, a_width=a_width, b_width=b_width)
    a_arrays, ub, gza = outs[:N_GROUPS], outs[N_GROUPS], outs[N_GROUPS + 1]
    bias = jnp.asarray(np.stack([_band_bias(False), _band_bias(True)]))
    ua = _attention(a_arrays, gza, bias, hm.astype(BF16), batch=batch, seq=seq, a_width=a_width)
    return _even_out(x2, ua, ub, w_out[:a_width].astype(BF16), w_out[a_width:].astype(BF16))


def kernel(x, positions, e_norm_w, e_w_in, e_q_norm_w, e_k_norm_w, e_conv_w, e_w_out, o_norm_w, o_w_in, o_pool_w, o_pool_scale, o_dconv_w, o_dconv_b, o_ln_w, o_ln_b, o_w_out):
    batch, seq, dm = x.shape
    x2 = x.reshape(batch * seq, dm)
    pos3 = positions.reshape(batch * seq // ROW_TILE, ROW_TILE // LANES, LANES)
    n_even, n_odd = e_norm_w.shape[0], o_norm_w.shape[0]
    for i in range(n_even + n_odd):
        j = i // 2
        if i % 2 == 0:
            x2 = _even_layer(x2, pos3, e_norm_w[j], e_w_in[j], e_q_norm_w[j], e_k_norm_w[j],
                             e_conv_w[j], e_w_out[j], batch=batch, seq=seq)
        else:
            c_width = d_width = dm // 2
            x2 = _odd_layer(x2, o_norm_w[j].reshape(1, dm), o_w_in[j].astype(BF16),
                            o_pool_w[j].astype(BF16), o_pool_scale[j].reshape(1, c_width),
                            o_dconv_w[j], o_dconv_b[j].reshape(1, d_width),
                            o_ln_w[j].reshape(1, d_width), o_ln_b[j].reshape(1, d_width),
                            o_w_out[j].astype(BF16), seq=seq, c_width=c_width, d_width=d_width)
    return x2.reshape(batch, seq, dm)
```
